```python
import jax, jax.numpy as jnp
from jax import lax
import numpy as np

D_MODEL = 1024
BATCH = 8
SEQ = 4096
DEPTH = 4

CHUNK = 64
Q_BLOCK = 128
PLE_DIM = 256

MLA_HEADS = 16
MLA_Q_LORA = D_MODEL // 2
MLA_KV_LORA = D_MODEL // 4
MLA_NOPE = 128
MLA_ROPE = 64
MLA_V = 128
ROPE_THETA = 10000.0

GLA_HEADS = 4
GLA_QK = D_MODEL // 2
GLA_VD = D_MODEL
GLA_DK = GLA_QK // GLA_HEADS
GLA_DV = GLA_VD // GLA_HEADS
GLA_GATE_RANK = 16
GLA_TAU = 16.0

D_FF = 2816
CONV_W = 3

N_MLA = (DEPTH + 1) // 2
N_GLA = DEPTH // 2
DN_ALPHA = (2 * DEPTH) ** 0.25
DN_BETA = (8 * DEPTH) ** -0.25
EPS = 1e-5
NEG_INF = -1e30

kernel_name = "hybrid_mla_gla_deepnorm_convffn_ple"


def layer_norm(x, g, b):
    xf = x.astype(jnp.float32)
    mu = jnp.mean(xf, -1, keepdims=True)
    var = jnp.mean(jnp.square(xf - mu), -1, keepdims=True)
    return ((xf - mu) * lax.rsqrt(var + EPS) * g + b).astype(x.dtype)


def rms_norm(x, g):
    xf = x.astype(jnp.float32)
    return (xf * lax.rsqrt(jnp.mean(jnp.square(xf), -1, keepdims=True) + EPS) * g).astype(x.dtype)


def rope_tables(positions):
    inv = 1.0 / (ROPE_THETA ** (jnp.arange(0, MLA_ROPE, 2, dtype=jnp.float32) / MLA_ROPE))
    ang = positions.astype(jnp.float32)[..., None] * inv
    return jnp.cos(ang), jnp.sin(ang)


def apply_rope(x, cos, sin):
    x1, x2 = jnp.split(x, 2, axis=-1)
    c = cos.astype(x.dtype)
    s = sin.astype(x.dtype)
    return jnp.concatenate([x1 * c - x2 * s, x1 * s + x2 * c], axis=-1)


def mla_mixer(x, positions, cos, sin, w_in, q_norm, kv_norm, w_uq, w_uk, w_uv, w_o):
    B, S, _ = x.shape
    h = x @ w_in
    c_q, c_kv, k_r = jnp.split(h, [MLA_Q_LORA, MLA_Q_LORA + MLA_KV_LORA], axis=-1)
    c_q = rms_norm(c_q, q_norm)
    c_kv = rms_norm(c_kv, kv_norm)
    q = (c_q @ w_uq).reshape(B, S, MLA_HEADS, MLA_NOPE + MLA_ROPE)
    q_nope = q[..., :MLA_NOPE]
    q_rope = apply_rope(q[..., MLA_NOPE:], cos[:, :, None], sin[:, :, None])
    k_rope = apply_rope(k_r, cos, sin)
    k_nope = (c_kv @ w_uk).reshape(B, S, MLA_HEADS, MLA_NOPE)
    v = (c_kv @ w_uv).reshape(B, S, MLA_HEADS, MLA_V)
    scale = (MLA_NOPE + MLA_ROPE) ** -0.5
    chunk_id = positions // CHUNK
    outs = []
    for blk in range(S // Q_BLOCK):
        q0 = blk * Q_BLOCK
        q1 = q0 + Q_BLOCK
        s = (jnp.einsum('bqhd,bkhd->bhqk', q_nope[:, q0:q1], k_nope[:, :q1])
             + jnp.einsum('bqhr,bkr->bhqk', q_rope[:, q0:q1], k_rope[:, :q1])).astype(jnp.float32) * scale
        mask = chunk_id[:, None, :q1] <= chunk_id[:, q0:q1, None]
        s = jnp.where(mask[:, None], s, NEG_INF)
        pr = jax.nn.softmax(s, axis=-1).astype(v.dtype)
        outs.append(jnp.einsum('bhqk,bkhd->bqhd', pr, v[:, :q1]))
    o = jnp.concatenate(outs, axis=1).reshape(B, S, MLA_HEADS * MLA_V)
    return (o @ w_o).astype(x.dtype)


def gla_mixer(x, w_in, w_a2, b_a, o_norm, w_o):
    B, S, _ = x.shape
    N = S // CHUNK
    h = x @ w_in
    q, k, v, r, a = jnp.split(h, [GLA_QK, 2 * GLA_QK, 2 * GLA_QK + GLA_VD, 2 * GLA_QK + 2 * GLA_VD], axis=-1)
    log_a = jax.nn.log_sigmoid((a @ w_a2 + b_a).astype(jnp.float32)) / GLA_TAU
    q = q.reshape(B, N, CHUNK, GLA_HEADS, GLA_DK) * (GLA_DK ** -0.5)
    k = k.reshape(B, N, CHUNK, GLA_HEADS, GLA_DK)
    v = v.reshape(B, N, CHUNK, GLA_HEADS, GLA_DV)
    log_a = log_a.reshape(B, N, CHUNK, GLA_HEADS, GLA_DK)
    cum = jnp.cumsum(log_a, axis=2)
    tot = cum[:, :, -1]
    k_dec = k * jnp.exp(tot[:, :, None] - cum).astype(k.dtype)
    upd = jnp.einsum('bnchk,bnchv->nbhkv', k_dec, v).astype(jnp.float32)
    decay = jnp.exp(jnp.moveaxis(tot, 1, 0))
    q_n = jnp.moveaxis(q, 1, 0)

    def step(state, inp):
        g, u, qc = inp
        state = state * g[..., None] + u
        return state, jnp.einsum('bchk,bhkv->bchv', qc, state)

    state0 = jnp.zeros((B, GLA_HEADS, GLA_DK, GLA_DV), jnp.float32)
    _, o = lax.scan(step, state0, (decay, upd, q_n))
    o = jnp.moveaxis(o, 0, 1).reshape(B, S, GLA_HEADS, GLA_DV)
    mu = jnp.mean(o, -1, keepdims=True)
    var = jnp.mean(jnp.square(o - mu), -1, keepdims=True)
    o = (o - mu) * lax.rsqrt(var + EPS) * o_norm.reshape(GLA_HEADS, GLA_DV)
    o = o.reshape(B, S, GLA_VD).astype(x.dtype) * jax.nn.silu(r)
    return (o @ w_o).astype(x.dtype)


def conv_ffn(x, w_up, conv_w, conv_b, w_down):
    S = x.shape[1]
    h = x @ w_up
    hp = jnp.pad(h, ((0, 0), (CONV_W - 1, 0), (0, 0)))
    h = hp[:, 0:S] * conv_w[0] + hp[:, 1:S + 1] * conv_w[1] + hp[:, 2:S + 2] * conv_w[2] + conv_b
    u, g = jnp.split(h, 2, axis=-1)
    return ((u * jax.nn.gelu(g)) @ w_down).astype(x.dtype)


def setup_inputs(seed: int = 0) -> dict:
    key = jax.random.key(seed)
    ks = iter(jax.random.split(key, 40))
    f32 = jnp.float32

    def nrm(shape, scale):
        return jax.random.normal(next(ks), shape, f32) * scale

    def gain(shape):
        return 1.0 + nrm(shape, 0.01)

    x = jax.random.normal(next(ks), (BATCH, SEQ, D_MODEL), f32)
    p = jax.random.normal(next(ks), (DEPTH, BATCH, SEQ, PLE_DIM), f32)
    offsets = jax.random.randint(next(ks), (BATCH, 1), 0, 16, dtype=jnp.int32) * CHUNK
    positions = (jnp.arange(SEQ, dtype=jnp.int32)[None, :] + offsets).astype(jnp.int32)

    mla_in = MLA_Q_LORA + MLA_KV_LORA + MLA_ROPE
    gla_in = 2 * GLA_QK + 2 * GLA_VD + GLA_GATE_RANK
    return {
        "x": x,
        "p": p,
        "positions": positions,
        "mla_w_in": nrm((N_MLA, D_MODEL, mla_in), D_MODEL ** -0.5),
        "mla_q_norm": gain((N_MLA, MLA_Q_LORA)),
        "mla_kv_norm": gain((N_MLA, MLA_KV_LORA)),
        "mla_w_uq": nrm((N_MLA, MLA_Q_LORA, MLA_HEADS * (MLA_NOPE + MLA_ROPE)), MLA_Q_LORA ** -0.5),
        "mla_w_uk": nrm((N_MLA, MLA_KV_LORA, MLA_HEADS * MLA_NOPE), MLA_KV_LORA ** -0.5),
        "mla_w_uv": nrm((N_MLA, MLA_KV_LORA, MLA_HEADS * MLA_V), DN_BETA * MLA_KV_LORA ** -0.5),
        "mla_w_o": nrm((N_MLA, MLA_HEADS * MLA_V, D_MODEL), DN_BETA * (MLA_HEADS * MLA_V) ** -0.5),
        "gla_w_in": nrm((N_GLA, D_MODEL, gla_in), D_MODEL ** -0.5),
        "gla_w_a2": nrm((N_GLA, GLA_GATE_RANK, GLA_QK), GLA_GATE_RANK ** -0.5),
        "gla_b_a": nrm((N_GLA, GLA_QK), 0.1),
        "gla_o_norm": gain((N_GLA, GLA_VD)),
        "gla_w_o": nrm((N_GLA, GLA_VD, D_MODEL), DN_BETA * GLA_VD ** -0.5),
        "ln1_g": gain((DEPTH, D_MODEL)),
        "ln1_b": nrm((DEPTH, D_MODEL), 0.01),
        "ln2_g": gain((DEPTH, D_MODEL)),
        "ln2_b": nrm((DEPTH, D_MODEL), 0.01),
        "ffn_w_up": nrm((DEPTH, D_MODEL, 2 * D_FF), D_MODEL ** -0.5),
        "ffn_conv_w": nrm((DEPTH, CONV_W, 2 * D_FF), CONV_W ** -0.5),
        "ffn_conv_b": nrm((DEPTH, 2 * D_FF), 0.01),
        "ffn_w_down": nrm((DEPTH, D_FF, D_MODEL), DN_BETA * D_FF ** -0.5),
        "ple_w_proj": nrm((DEPTH, PLE_DIM, D_MODEL), PLE_DIM ** -0.5),
        "ple_w_gate": nrm((DEPTH, D_MODEL, D_MODEL), D_MODEL ** -0.5),
        "ple_b_gate": nrm((DEPTH, D_MODEL), 0.01),
    }


def reference(x, p, positions, mla_w_in, mla_q_norm, mla_kv_norm, mla_w_uq, mla_w_uk, mla_w_uv, mla_w_o,
              gla_w_in, gla_w_a2, gla_b_a, gla_o_norm, gla_w_o, ln1_g, ln1_b, ln2_g, ln2_b,
              ffn_w_up, ffn_conv_w, ffn_conv_b, ffn_w_down, ple_w_proj, ple_w_gate, ple_b_gate):
    cos, sin = rope_tables(positions)
    for i in range(DEPTH):
        j = i // 2
        if i % 2 == 0:
            m = mla_mixer(x, positions, cos, sin, mla_w_in[j], mla_q_norm[j], mla_kv_norm[j],
                          mla_w_uq[j], mla_w_uk[j], mla_w_uv[j], mla_w_o[j])
        else:
            m = gla_mixer(x, gla_w_in[j], gla_w_a2[j], gla_b_a[j], gla_o_norm[j], gla_w_o[j])
        x = layer_norm(DN_ALPHA * x + m, ln1_g[i], ln1_b[i])
        x = layer_norm(DN_ALPHA * x + conv_ffn(x, ffn_w_up[i], ffn_conv_w[i], ffn_conv_b[i], ffn_w_down[i]),
                       ln2_g[i], ln2_b[i])
        gate = jax.nn.sigmoid(x @ ple_w_gate[i] + ple_b_gate[i])
        x = x + gate * (p[i] @ ple_w_proj[i])
    return x
```

```python
import functools
import math

import jax
import jax.numpy as jnp
from jax import lax
from jax.experimental import pallas as pl
from jax.experimental.pallas import tpu as pltpu

CHUNK = 64
CHUNK_SHIFT = 6
MLA_HEADS = 16
MLA_Q_LORA = 512
MLA_KV_LORA = 256
MLA_NOPE = 128
MLA_ROPE = 64
MLA_V = 128
MLA_QK = MLA_NOPE + MLA_ROPE
ROPE_THETA = 10000.0
GLA_HEADS = 4
GLA_QK = 512
GLA_VD = 1024
GLA_DK = GLA_QK // GLA_HEADS
GLA_DV = GLA_VD // GLA_HEADS
GLA_GATE_RANK = 16
GLA_TAU = 16.0
D_FF = 2816
DEPTH = 4
DN_ALPHA = (2 * DEPTH) ** 0.25
EPS = 1e-5
NEG_INF = -1e30

LANES = 128
SUBLANES = 8
VMEM_LIMIT = 56 * 1024 * 1024

BF16 = jnp.bfloat16
F32 = jnp.float32


def _dot(a, b):
    return jnp.dot(a, b, preferred_element_type=F32)


def _dot_nt(a, b):
    return lax.dot_general(a, b, (((1,), (1,)), ((), ())), preferred_element_type=F32)


def _dot_tn(a, b):
    return lax.dot_general(a, b, (((0,), (0,)), ((), ())), preferred_element_type=F32)


def _layer_norm(y, g, b):
    mu = jnp.mean(y, -1, keepdims=True)
    d = y - mu
    var = jnp.mean(d * d, -1, keepdims=True)
    return d * lax.rsqrt(var + EPS) * g + b


def _rms_norm(y, g):
    return y * lax.rsqrt(jnp.mean(y * y, -1, keepdims=True) + EPS) * g


def _const_spec(shape):
    nd = len(shape)
    return pl.BlockSpec(shape, lambda *_: (0,) * nd, pipeline_mode=pl.Buffered(1))


def _params(sem):
    return pltpu.CompilerParams(dimension_semantics=sem, vmem_limit_bytes=VMEM_LIMIT)


def _rope128(seg, cos, sin_signed, first_half):
    fwd = pltpu.roll(seg, LANES - MLA_ROPE // 2, axis=1)
    bwd = pltpu.roll(seg, MLA_ROPE // 2, axis=1)
    return seg * cos + jnp.where(first_half, fwd, bwd) * sin_signed


def _mla_proj_kernel(x_ref, pos_ref, invf_ref, w_in_ref, qn_g_ref, kv_g_ref, w_qn_ref, w_qr_ref,
                     w_uk_ref, w_uv_ref, q_ref, k_ref, v_ref, *, q_scale):
    xb = x_ref[0].astype(BF16)
    h = _dot(xb, w_in_ref[...])
    c_q = _rms_norm(h[:, :MLA_Q_LORA], qn_g_ref[...]).astype(BF16)
    c_kv = _rms_norm(h[:, MLA_Q_LORA:MLA_Q_LORA + MLA_KV_LORA], kv_g_ref[...]).astype(BF16)
    k_r = h[:, MLA_Q_LORA + MLA_KV_LORA:]

    ang = pos_ref[0] * invf_ref[...]
    cos = jnp.cos(ang)
    lane = lax.broadcasted_iota(jnp.int32, (1, LANES), 1)
    first_half = (lane & (MLA_ROPE - 1)) < MLA_ROPE // 2
    sin_signed = jnp.where(first_half, -jnp.sin(ang), jnp.sin(ang))

    k_rope = _rope128(k_r, cos, sin_signed, first_half)[:, :MLA_ROPE].astype(BF16)
    qn = _dot(c_q, w_qn_ref[...]) * q_scale
    qr = _dot(c_q, w_qr_ref[...]) * q_scale
    kn = _dot(c_kv, w_uk_ref[...])
    vv = _dot(c_kv, w_uv_ref[...])
    for hp in range(MLA_HEADS // 2):
        seg = _rope128(qr[:, hp * LANES:(hp + 1) * LANES], cos, sin_signed, first_half).astype(BF16)
        for sub in range(2):
            hd = 2 * hp + sub
            q_ref[0, hd, :, :MLA_NOPE] = qn[:, hd * MLA_NOPE:(hd + 1) * MLA_NOPE].astype(BF16)
            q_ref[0, hd, :, MLA_NOPE:] = seg[:, sub * MLA_ROPE:(sub + 1) * MLA_ROPE]
            k_ref[0, hd, :, :MLA_NOPE] = kn[:, hd * MLA_NOPE:(hd + 1) * MLA_NOPE].astype(BF16)
            k_ref[0, hd, :, MLA_NOPE:] = k_rope
            v_ref[0, hd] = vv[:, hd * MLA_V:(hd + 1) * MLA_V].astype(BF16)


def _mla_proj(x, posf, invf, w_in, qn_g, kv_g, w_qn, w_qr, w_uk, w_uv, *, tm):
    B, S, D = x.shape
    H = MLA_HEADS
    q_scale = (MLA_QK ** -0.5) * math.log2(math.e)
    row = lambda b, i: (b, i, 0)
    head = lambda b, i: (b, 0, i, 0)
    return pl.pallas_call(
        functools.partial(_mla_proj_kernel, q_scale=q_scale),
        grid=(B, S // tm),
        in_specs=[pl.BlockSpec((1, tm, D), row), pl.BlockSpec((1, tm, 1), row),
                  _const_spec(invf.shape), _const_spec(w_in.shape), _const_spec(qn_g.shape),
                  _const_spec(kv_g.shape), _const_spec(w_qn.shape), _const_spec(w_qr.shape),
                  _const_spec(w_uk.shape), _const_spec(w_uv.shape)],
        out_specs=[pl.BlockSpec((1, H, tm, MLA_QK), head), pl.BlockSpec((1, H, tm, MLA_QK), head),
                   pl.BlockSpec((1, H, tm, MLA_V), head)],
        out_shape=[jax.ShapeDtypeStruct((B, H, S, MLA_QK), BF16),
                   jax.ShapeDtypeStruct((B, H, S, MLA_QK), BF16),
                   jax.ShapeDtypeStruct((B, H, S, MLA_V), BF16)],
        compiler_params=_params(("parallel", "parallel")),
        name="mla_proj",
    )(x, posf, invf, w_in, qn_g, kv_g, w_qn, w_qr, w_uk, w_uv)


def _mla_attn_kernel(q_ref, k_ref, v_ref, pq_ref, pk_ref, o_ref, *, tq, nq):
    def q_tile(qi, _):
        q0 = pl.multiple_of(qi * tq, tq)
        q = q_ref[0, 0, pl.ds(q0, tq), :]

        def kv_tile(k0, carry, mask):
            m, l, acc = carry
            k = k_ref[0, 0, pl.ds(k0, tq), :]
            v = v_ref[0, 0, pl.ds(k0, tq), :]
            s = _dot_nt(q, k)
            if mask is not None:
                s = jnp.where(mask, s, NEG_INF)
            m_new = jnp.maximum(m, jnp.max(s, -1, keepdims=True))
            alpha = jnp.exp2(m - m_new)
            p = jnp.exp2(s - m_new)
            l = alpha * l + jnp.sum(p, -1, keepdims=True)
            acc = alpha * acc + _dot(p.astype(BF16), v)
            return m_new, l, acc

        init = (jnp.full((tq, 1), NEG_INF, F32), jnp.zeros((tq, 1), F32), jnp.zeros((tq, MLA_V), F32))
        carry = lax.fori_loop(0, qi, lambda j, c: kv_tile(pl.multiple_of(j * tq, tq), c, None), init)
        q_chunk = lax.shift_right_arithmetic(pq_ref[0, qi], CHUNK_SHIFT)
        k_chunk = lax.shift_right_arithmetic(pk_ref[0, qi], CHUNK_SHIFT)
        m, l, acc = kv_tile(q0, carry, k_chunk <= q_chunk)
        o_ref[0, pl.ds(q0, tq), :] = (acc / l).astype(o_ref.dtype)
        return 0

    lax.fori_loop(0, nq, q_tile, 0)


def _mla_attn(q, k, v, pos, *, tq):
    B, H, S, _ = q.shape
    nq = S // tq
    pq = pos.reshape(B, nq, tq, 1)
    pk = pos.reshape(B, nq, 1, tq)
    head = lambda b, h: (b, h, 0, 0)
    batch = lambda b, h: (b, 0, 0, 0)
    return pl.pallas_call(
        functools.partial(_mla_attn_kernel, tq=tq, nq=nq),
        grid=(B, H),
        in_specs=[pl.BlockSpec((1, 1, S, MLA_QK), head), pl.BlockSpec((1, 1, S, MLA_QK), head),
                  pl.BlockSpec((1, 1, S, MLA_V), head),
                  pl.BlockSpec((1, nq, tq, 1), batch), pl.BlockSpec((1, nq, 1, tq), batch)],
        out_specs=pl.BlockSpec((1, S, MLA_V), lambda b, h: (b, 0, h)),
        out_shape=jax.ShapeDtypeStruct((B, S, H * MLA_V), BF16),
        compiler_params=_params(("parallel", "parallel")),
        name="mla_attn",
    )(q, k, v, pq, pk)


def _gla_proj_kernel(x_ref, w_q_ref, w_k_ref, w_v_ref, w_r_ref, w_a_ref, w_a2_ref, b_a_ref,
                     q_ref, k_ref, v_ref, r_ref, la_ref):
    xb = x_ref[0].astype(BF16)
    q_ref[0] = (_dot(xb, w_q_ref[...]) * (GLA_DK ** -0.5)).astype(BF16)
    k_ref[0] = _dot(xb, w_k_ref[...])
    v_ref[0] = _dot(xb, w_v_ref[...]).astype(BF16)
    r_ref[0] = _dot(xb, w_r_ref[...])
    a = _dot(xb, w_a_ref[...]).astype(BF16)
    z = _dot(a, w_a2_ref[...]) + b_a_ref[...]
    log_sig = jnp.minimum(z, 0.0) - jnp.log1p(jnp.exp(-jnp.abs(z)))
    la_ref[0] = log_sig * (1.0 / GLA_TAU)


def _gla_proj(x, w_q, w_k, w_v, w_r, w_a, w_a2, b_a, *, tm):
    B, S, D = x.shape
    row = lambda b, i: (b, i, 0)
    outs = [(GLA_QK, BF16), (GLA_QK, F32), (GLA_VD, BF16), (GLA_VD, F32), (GLA_QK, F32)]
    return pl.pallas_call(
        _gla_proj_kernel,
        grid=(B, S // tm),
        in_specs=[pl.BlockSpec((1, tm, D), row)] + [_const_spec(w.shape) for w in
                                                    (w_q, w_k, w_v, w_r, w_a, w_a2, b_a)],
        out_specs=[pl.BlockSpec((1, tm, n), row) for n, _ in outs],
        out_shape=[jax.ShapeDtypeStruct((B, S, n), dt) for n, dt in outs],
        compiler_params=_params(("parallel", "parallel")),
        name="gla_proj",
    )(x, w_q, w_k, w_v, w_r, w_a, w_a2, b_a)


def _gla_chunk_kernel(q_ref, k_ref, v_ref, r_ref, la_ref, on_ref, o_ref, state_ref, *, n_chunks):
    @pl.when(pl.program_id(1) == 0)
    def _():
        state_ref[...] = jnp.zeros_like(state_ref)

    ri = lax.broadcasted_iota(jnp.int32, (CHUNK, CHUNK), 0)
    ci = lax.broadcasted_iota(jnp.int32, (CHUNK, CHUNK), 1)
    tri = (ci <= ri).astype(F32)

    for c in range(n_chunks):
        rows = pl.ds(c * CHUNK, CHUNK)
        la = la_ref[0, rows, :]
        cum = jnp.dot(tri, la, precision=lax.Precision.HIGHEST, preferred_element_type=F32)
        tot = cum[CHUNK - 1:CHUNK, :]
        k_dec = (k_ref[0, rows, :] * jnp.exp(tot - cum)).astype(BF16)
        decay = jnp.exp(tot)
        for hd in range(GLA_HEADS):
            ks = slice(hd * GLA_DK, (hd + 1) * GLA_DK)
            vs = slice(hd * GLA_DV, (hd + 1) * GLA_DV)
            upd_t = _dot_tn(v_ref[0, rows, vs], k_dec[:, ks])
            st = state_ref[hd] * decay[:, ks] + upd_t
            state_ref[hd] = st
            o = _dot_nt(q_ref[0, rows, ks], st.astype(BF16))
            mu = jnp.mean(o, -1, keepdims=True)
            d = o - mu
            var = jnp.mean(d * d, -1, keepdims=True)
            o = d * lax.rsqrt(var + EPS) * on_ref[:, vs]
            r = r_ref[0, rows, vs]
            o_ref[0, rows, vs] = (o * (r * jax.nn.sigmoid(r))).astype(o_ref.dtype)


def _gla_chunk(q, k, v, r, la, o_norm, *, tc):
    B, S, _ = q.shape
    row = lambda b, i: (b, i, 0)
    return pl.pallas_call(
        functools.partial(_gla_chunk_kernel, n_chunks=tc // CHUNK),
        grid=(B, S // tc),
        in_specs=[pl.BlockSpec((1, tc, GLA_QK), row), pl.BlockSpec((1, tc, GLA_QK), row),
                  pl.BlockSpec((1, tc, GLA_VD), row), pl.BlockSpec((1, tc, GLA_VD), row),
                  pl.BlockSpec((1, tc, GLA_QK), row), _const_spec(o_norm.shape)],
        out_specs=pl.BlockSpec((1, tc, GLA_VD), row),
        out_shape=jax.ShapeDtypeStruct((B, S, GLA_VD), BF16),
        scratch_shapes=[pltpu.VMEM((GLA_HEADS, GLA_DV, GLA_DK), F32)],
        compiler_params=_params(("parallel", "arbitrary")),
        name="gla_chunk",
    )(q, k, v, r, la, o_norm)


def _out_ln_kernel(o_ref, w_ref, x_ref, g_ref, b_ref, y_ref):
    m = _dot(o_ref[...], w_ref[...])
    y_ref[...] = _layer_norm(DN_ALPHA * x_ref[...] + m, g_ref[...], b_ref[...])


def _out_ln(o, w, x, g, b, *, tm):
    T, K = o.shape
    D = x.shape[-1]
    row = lambda i: (i, 0)
    return pl.pallas_call(
        _out_ln_kernel,
        grid=(T // tm,),
        in_specs=[pl.BlockSpec((tm, K), row), _const_spec(w.shape), pl.BlockSpec((tm, D), row),
                  _const_spec(g.shape), _const_spec(b.shape)],
        out_specs=pl.BlockSpec((tm, D), row),
        out_shape=jax.ShapeDtypeStruct((T, D), F32),
        compiler_params=_params(("parallel",)),
        name="out_ln",
    )(o, w, x, g, b)


def _causal_conv3(h, halo, cw, cb):
    w0, w1, w2 = cw[0:1], cw[1:2], cw[2:3]
    y = pltpu.roll(h, 2, axis=0) * w0 + pltpu.roll(h, 1, axis=0) * w1 + h * w2 + cb
    head = jnp.concatenate([halo, h[:SUBLANES]], axis=0)
    y_head = (pltpu.roll(head, 2, axis=0) * w0 + pltpu.roll(head, 1, axis=0) * w1 + head * w2 + cb)
    return jnp.concatenate([y_head[SUBLANES:], y[SUBLANES:]], axis=0)


def _gelu_tanh(g):
    c = math.sqrt(2.0 / math.pi)
    return 0.5 * g * (1.0 + jnp.tanh(c * (g + 0.044715 * (g * g * g))))


def _ffn_ple_kernel(x_ref, p_ref, wu_ref, wg_ref, cwu_ref, cwg_ref, cbu_ref, cbg_ref, wd_ref,
                    g_ref, b_ref, wgate_ref, bgate_ref, wproj_ref, y_ref,
                    acc_ref, halo_u_ref, halo_g_ref, *, n_ff):
    x = x_ref[0]
    xb = x.astype(BF16)
    tm = x.shape[0]

    @pl.when(pl.program_id(1) == 0)
    def _():
        halo_u_ref[...] = jnp.zeros_like(halo_u_ref)
        halo_g_ref[...] = jnp.zeros_like(halo_g_ref)

    def ff_chunk(c, _):
        hu = _dot(xb, wu_ref[c])
        hg = _dot(xb, wg_ref[c])
        halo_u = halo_u_ref[c]
        halo_g = halo_g_ref[c]
        halo_u_ref[c] = hu[tm - SUBLANES:]
        halo_g_ref[c] = hg[tm - SUBLANES:]
        u = _causal_conv3(hu, halo_u, cwu_ref[c], cbu_ref[c])
        g = _causal_conv3(hg, halo_g, cwg_ref[c], cbg_ref[c])
        part = _dot((u * _gelu_tanh(g)).astype(BF16), wd_ref[c])

        @pl.when(c == 0)
        def _():
            acc_ref[...] = part

        @pl.when(c > 0)
        def _():
            acc_ref[...] += part
        return 0

    lax.fori_loop(0, n_ff, ff_chunk, 0)
    x2 = _layer_norm(DN_ALPHA * x + acc_ref[...], g_ref[...], b_ref[...])
    gate = jax.nn.sigmoid(_dot(x2.astype(BF16), wgate_ref[...]) + bgate_ref[...])
    y_ref[0] = x2 + gate * _dot(p_ref[0].astype(BF16), wproj_ref[...])


def _ffn_ple(x, p, wu, wg, cwu, cwg, cbu, cbg, wd, g, b, wgate, bgate, wproj, *, tm):
    B, S, D = x.shape
    n_ff, _, ck = wu.shape
    row = lambda bi, i: (bi, i, 0)
    consts = (wu, wg, cwu, cwg, cbu, cbg, wd, g, b, wgate, bgate, wproj)
    return pl.pallas_call(
        functools.partial(_ffn_ple_kernel, n_ff=n_ff),
        grid=(B, S // tm),
        in_specs=[pl.BlockSpec((1, tm, D), row), pl.BlockSpec((1, tm, p.shape[-1]), row)]
                 + [_const_spec(w.shape) for w in consts],
        out_specs=pl.BlockSpec((1, tm, D), row),
        out_shape=jax.ShapeDtypeStruct((B, S, D), F32),
        scratch_shapes=[pltpu.VMEM((tm, D), F32), pltpu.VMEM((n_ff, SUBLANES, ck), F32),
                        pltpu.VMEM((n_ff, SUBLANES, ck), F32)],
        compiler_params=_params(("parallel", "arbitrary")),
        name="ffn_ple",
    )(x, p, *consts)


FF_CHUNK = 256
TM_MLA_PROJ = 256
TQ_ATTN = 512
TM_GLA_PROJ = 512
TC_GLA = 512
TM_OUT = 512
TM_FFN = 512


def _chunk_cols(w, n):
    K = w.shape[0]
    return w.reshape(K, n, -1).transpose(1, 0, 2)


def kernel(x, p, positions, mla_w_in, mla_q_norm, mla_kv_norm, mla_w_uq, mla_w_uk, mla_w_uv, mla_w_o, gla_w_in, gla_w_a2, gla_b_a, gla_o_norm, gla_w_o, ln1_g, ln1_b, ln2_g, ln2_b, ffn_w_up, ffn_conv_w, ffn_conv_b, ffn_w_down, ple_w_proj, ple_w_gate, ple_b_gate):
    B, S, D = x.shape
    T = B * S
    posf = positions.astype(F32).reshape(B, S, 1)
    inv = 1.0 / (ROPE_THETA ** (jnp.arange(0, MLA_ROPE, 2, dtype=F32) / MLA_ROPE))
    invf = jnp.tile(inv, LANES // inv.shape[0]).reshape(1, LANES)
    n_ff = D_FF // FF_CHUNK

    for i in range(DEPTH):
        j = i // 2
        if i % 2 == 0:
            w_in = jnp.pad(mla_w_in[j], ((0, 0), (0, LANES - MLA_ROPE))).astype(BF16)
            w_uq = mla_w_uq[j].reshape(MLA_Q_LORA, MLA_HEADS, MLA_QK)
            w_qn = w_uq[:, :, :MLA_NOPE].reshape(MLA_Q_LORA, -1).astype(BF16)
            w_qr = w_uq[:, :, MLA_NOPE:].reshape(MLA_Q_LORA, -1).astype(BF16)
            q, k, v = _mla_proj(x, posf, invf, w_in, mla_q_norm[j].reshape(1, -1),
                                mla_kv_norm[j].reshape(1, -1), w_qn, w_qr,
                                mla_w_uk[j].astype(BF16), mla_w_uv[j].astype(BF16), tm=TM_MLA_PROJ)
            o = _mla_attn(q, k, v, positions, tq=TQ_ATTN)
            w_o = mla_w_o[j]
        else:
            w = gla_w_in[j]
            w_q, w_k, w_v, w_r, w_a = jnp.split(
                w, [GLA_QK, 2 * GLA_QK, 2 * GLA_QK + GLA_VD, 2 * GLA_QK + 2 * GLA_VD], axis=1)
            w_a = jnp.pad(w_a, ((0, 0), (0, LANES - GLA_GATE_RANK)))
            w_a2 = jnp.pad(gla_w_a2[j], ((0, LANES - GLA_GATE_RANK), (0, 0)))
            q, k, v, r, la = _gla_proj(x, w_q.astype(BF16), w_k.astype(BF16), w_v.astype(BF16),
                                       w_r.astype(BF16), w_a.astype(BF16), w_a2.astype(BF16),
                                       gla_b_a[j].reshape(1, -1), tm=TM_GLA_PROJ)
            o = _gla_chunk(q, k, v, r, la, gla_o_norm[j].reshape(1, -1), tc=TC_GLA)
            w_o = gla_w_o[j]
        x = _out_ln(o.reshape(T, -1), w_o.astype(BF16), x.reshape(T, D), ln1_g[i].reshape(1, -1),
                    ln1_b[i].reshape(1, -1), tm=TM_OUT).reshape(B, S, D)

        w_up = ffn_w_up[i].astype(BF16)
        cw, cb = ffn_conv_w[i], ffn_conv_b[i].reshape(1, -1)
        x = _ffn_ple(x, p[i],
                     _chunk_cols(w_up[:, :D_FF], n_ff), _chunk_cols(w_up[:, D_FF:], n_ff),
                     _chunk_cols(cw[:, :D_FF], n_ff), _chunk_cols(cw[:, D_FF:], n_ff),
                     _chunk_cols(cb[:, :D_FF], n_ff), _chunk_cols(cb[:, D_FF:], n_ff),
                     ffn_w_down[i].astype(BF16).reshape(n_ff, FF_CHUNK, D),
                     ln2_g[i].reshape(1, -1), ln2_b[i].reshape(1, -1),
                     ple_w_gate[i].astype(BF16), ple_b_gate[i].reshape(1, -1),
                     ple_w_proj[i].astype(BF16), tm=TM_FFN)
    return x
```

```python
import functools
import math

import jax
import jax.numpy as jnp
from jax import lax
from jax.experimental import pallas as pl
from jax.experimental.pallas import tpu as pltpu

CHUNK = 64
CHUNK_SHIFT = 6
MLA_HEADS = 16
MLA_Q_LORA = 512
MLA_KV_LORA = 256
MLA_NOPE = 128
MLA_ROPE = 64
MLA_V = 128
MLA_QK = MLA_NOPE + MLA_ROPE
ROPE_THETA = 10000.0
GLA_HEADS = 4
GLA_QK = 512
GLA_VD = 1024
GLA_DK = GLA_QK // GLA_HEADS
GLA_DV = GLA_VD // GLA_HEADS
GLA_GATE_RANK = 16
GLA_TAU = 16.0
D_FF = 2816
DEPTH = 4
DN_ALPHA = (2 * DEPTH) ** 0.25
EPS = 1e-5
NEG_INF = -1e30

LANES = 128
SUBLANES = 8
VMEM_LIMIT = 56 * 1024 * 1024

BF16 = jnp.bfloat16
F32 = jnp.float32


def _dot(a, b):
    return jnp.dot(a, b, preferred_element_type=F32)


def _dot_nt(a, b):
    return lax.dot_general(a, b, (((1,), (1,)), ((), ())), preferred_element_type=F32)


def _dot_tn(a, b):
    return lax.dot_general(a, b, (((0,), (0,)), ((), ())), preferred_element_type=F32)


def _layer_norm(y, g, b):
    mu = jnp.mean(y, -1, keepdims=True)
    d = y - mu
    var = jnp.mean(d * d, -1, keepdims=True)
    return d * lax.rsqrt(var + EPS) * g + b


def _rms_norm(y, g):
    return y * lax.rsqrt(jnp.mean(y * y, -1, keepdims=True) + EPS) * g


def _const_spec(shape):
    nd = len(shape)
    return pl.BlockSpec(shape, lambda *_: (0,) * nd, pipeline_mode=pl.Buffered(1))


def _params(sem):
    return pltpu.CompilerParams(dimension_semantics=sem, vmem_limit_bytes=VMEM_LIMIT)


def _rope128(seg, cos, sin_signed, first_half):
    fwd = pltpu.roll(seg, LANES - MLA_ROPE // 2, axis=1)
    bwd = pltpu.roll(seg, MLA_ROPE // 2, axis=1)
    return seg * cos + jnp.where(first_half, fwd, bwd) * sin_signed


def _mla_proj_kernel(x_ref, pos_ref, invf_ref, w_in_ref, qn_g_ref, kv_g_ref, w_qn_ref, w_qr_ref,
                     w_uk_ref, w_uv_ref, q_ref, k_ref, v_ref, *, q_scale):
    xb = x_ref[0].astype(BF16)
    h = _dot(xb, w_in_ref[...])
    c_q = _rms_norm(h[:, :MLA_Q_LORA], qn_g_ref[...]).astype(BF16)
    c_kv = _rms_norm(h[:, MLA_Q_LORA:MLA_Q_LORA + MLA_KV_LORA], kv_g_ref[...]).astype(BF16)
    k_r = h[:, MLA_Q_LORA + MLA_KV_LORA:]

    ang = pos_ref[0] * invf_ref[...]
    cos = jnp.cos(ang)
    lane = lax.broadcasted_iota(jnp.int32, (1, LANES), 1)
    first_half = (lane & (MLA_ROPE - 1)) < MLA_ROPE // 2
    sin_signed = jnp.where(first_half, -jnp.sin(ang), jnp.sin(ang))

    k_rope = _rope128(k_r, cos, sin_signed, first_half)[:, :MLA_ROPE].astype(BF16)
    qn = _dot(c_q, w_qn_ref[...]) * q_scale
    qr = _dot(c_q, w_qr_ref[...]) * q_scale
    kn = _dot(c_kv, w_uk_ref[...])
    vv = _dot(c_kv, w_uv_ref[...])
    for hp in range(MLA_HEADS // 2):
        seg = _rope128(qr[:, hp * LANES:(hp + 1) * LANES], cos, sin_signed, first_half).astype(BF16)
        for sub in range(2):
            hd = 2 * hp + sub
            q_ref[0, hd, :, :MLA_NOPE] = qn[:, hd * MLA_NOPE:(hd + 1) * MLA_NOPE].astype(BF16)
            q_ref[0, hd, :, MLA_NOPE:] = seg[:, sub * MLA_ROPE:(sub + 1) * MLA_ROPE]
            k_ref[0, hd, :, :MLA_NOPE] = kn[:, hd * MLA_NOPE:(hd + 1) * MLA_NOPE].astype(BF16)
            k_ref[0, hd, :, MLA_NOPE:] = k_rope
            v_ref[0, hd] = vv[:, hd * MLA_V:(hd + 1) * MLA_V].astype(BF16)


def _mla_proj(x, posf, invf, w_in, qn_g, kv_g, w_qn, w_qr, w_uk, w_uv, *, tm):
    B, S, D = x.shape
    H = MLA_HEADS
    q_scale = (MLA_QK ** -0.5) * math.log2(math.e)
    row = lambda b, i: (b, i, 0)
    head = lambda b, i: (b, 0, i, 0)
    return pl.pallas_call(
        functools.partial(_mla_proj_kernel, q_scale=q_scale),
        grid=(B, S // tm),
        in_specs=[pl.BlockSpec((1, tm, D), row), pl.BlockSpec((1, tm, 1), row),
                  _const_spec(invf.shape), _const_spec(w_in.shape), _const_spec(qn_g.shape),
                  _const_spec(kv_g.shape), _const_spec(w_qn.shape), _const_spec(w_qr.shape),
                  _const_spec(w_uk.shape), _const_spec(w_uv.shape)],
        out_specs=[pl.BlockSpec((1, H, tm, MLA_QK), head), pl.BlockSpec((1, H, tm, MLA_QK), head),
                   pl.BlockSpec((1, H, tm, MLA_V), head)],
        out_shape=[jax.ShapeDtypeStruct((B, H, S, MLA_QK), BF16),
                   jax.ShapeDtypeStruct((B, H, S, MLA_QK), BF16),
                   jax.ShapeDtypeStruct((B, H, S, MLA_V), BF16)],
        compiler_params=_params(("parallel", "parallel")),
        name="mla_proj",
    )(x, posf, invf, w_in, qn_g, kv_g, w_qn, w_qr, w_uk, w_uv)


def _mla_attn_kernel(q_ref, k_ref, v_ref, pq_ref, pk_ref, o_ref, m_ref, l_ref, acc_ref, *, tq, hg):
    qi = pl.program_id(2)
    m_ref[...] = jnp.full_like(m_ref, NEG_INF)
    l_ref[...] = jnp.zeros_like(l_ref)
    acc_ref[...] = jnp.zeros_like(acc_ref)

    def kv_tile(k0, mask):
        scores = [_dot_nt(q_ref[0, hh], k_ref[0, hh, pl.ds(k0, tq), :]) for hh in range(hg)]
        for hh in range(hg):
            v = v_ref[0, hh, pl.ds(k0, tq), :]
            s = scores[hh]
            if mask is not None:
                s = jnp.where(mask, s, NEG_INF)
            cols = [s[:, j * LANES:(j + 1) * LANES] for j in range(tq // LANES)]
            m = m_ref[hh]
            m_new = jnp.maximum(m, jnp.max(functools.reduce(jnp.maximum, cols), -1, keepdims=True))
            alpha = jnp.exp2(m - m_new)
            ps = [jnp.exp2(c - m_new) for c in cols]
            m_ref[hh] = m_new
            l_ref[hh] = alpha * l_ref[hh] + jnp.sum(functools.reduce(jnp.add, ps), -1, keepdims=True)
            p = jnp.concatenate([c.astype(BF16) for c in ps], axis=1)
            acc_ref[hh] = alpha * acc_ref[hh] + _dot(p, v)

    def full_tile(j, _):
        kv_tile(pl.multiple_of(j * tq, tq), None)
        return 0

    lax.fori_loop(0, qi, full_tile, 0)
    q_chunk = lax.shift_right_arithmetic(pq_ref[0, 0], CHUNK_SHIFT)
    k_chunk = lax.shift_right_arithmetic(pk_ref[0, 0], CHUNK_SHIFT)
    kv_tile(pl.multiple_of(qi * tq, tq), k_chunk <= q_chunk)
    for hh in range(hg):
        o_ref[0, :, hh * MLA_V:(hh + 1) * MLA_V] = (acc_ref[hh] / l_ref[hh]).astype(o_ref.dtype)


def _mla_attn(q, k, v, pos, *, tq, hg):
    B, H, S, _ = q.shape
    nq = S // tq
    pq = pos.reshape(B, nq, tq, 1)
    pk = pos.reshape(B, nq, 1, tq)
    kv_spec = lambda d: pl.BlockSpec((1, hg, S, d), lambda b, h, i: (b, h, 0, 0))
    return pl.pallas_call(
        functools.partial(_mla_attn_kernel, tq=tq, hg=hg),
        grid=(B, H // hg, nq),
        in_specs=[pl.BlockSpec((1, hg, tq, MLA_QK), lambda b, h, i: (b, h, i, 0)),
                  kv_spec(MLA_QK), kv_spec(MLA_V),
                  pl.BlockSpec((1, 1, tq, 1), lambda b, h, i: (b, i, 0, 0)),
                  pl.BlockSpec((1, 1, 1, tq), lambda b, h, i: (b, i, 0, 0))],
        out_specs=pl.BlockSpec((1, tq, hg * MLA_V), lambda b, h, i: (b, i, h)),
        out_shape=jax.ShapeDtypeStruct((B, S, H * MLA_V), BF16),
        scratch_shapes=[pltpu.VMEM((hg, tq, LANES), F32), pltpu.VMEM((hg, tq, LANES), F32),
                        pltpu.VMEM((hg, tq, MLA_V), F32)],
        compiler_params=_params(("parallel", "parallel", "arbitrary")),
        name="mla_attn",
    )(q, k, v, pq, pk)


def _gla_proj_kernel(x_ref, w_q_ref, w_k_ref, w_v_ref, w_r_ref, w_a_ref, w_a2_ref, b_a_ref,
                     q_ref, k_ref, v_ref, r_ref, la_ref):
    xb = x_ref[0].astype(BF16)
    q_ref[0] = (_dot(xb, w_q_ref[...]) * (GLA_DK ** -0.5)).astype(BF16)
    k_ref[0] = _dot(xb, w_k_ref[...])
    v_ref[0] = _dot(xb, w_v_ref[...]).astype(BF16)
    r_ref[0] = _dot(xb, w_r_ref[...])
    a = _dot(xb, w_a_ref[...]).astype(BF16)
    z = _dot(a, w_a2_ref[...]) + b_a_ref[...]
    log_sig = jnp.minimum(z, 0.0) - jnp.log1p(jnp.exp(-jnp.abs(z)))
    la_ref[0] = log_sig * (1.0 / GLA_TAU)


def _gla_proj(x, w_q, w_k, w_v, w_r, w_a, w_a2, b_a, *, tm):
    B, S, D = x.shape
    row = lambda b, i: (b, i, 0)
    outs = [(GLA_QK, BF16), (GLA_QK, F32), (GLA_VD, BF16), (GLA_VD, F32), (GLA_QK, F32)]
    return pl.pallas_call(
        _gla_proj_kernel,
        grid=(B, S // tm),
        in_specs=[pl.BlockSpec((1, tm, D), row)] + [_const_spec(w.shape) for w in
                                                    (w_q, w_k, w_v, w_r, w_a, w_a2, b_a)],
        out_specs=[pl.BlockSpec((1, tm, n), row) for n, _ in outs],
        out_shape=[jax.ShapeDtypeStruct((B, S, n), dt) for n, dt in outs],
        compiler_params=_params(("parallel", "parallel")),
        name="gla_proj",
    )(x, w_q, w_k, w_v, w_r, w_a, w_a2, b_a)


def _gla_chunk_kernel(q_ref, k_ref, v_ref, r_ref, la_ref, on_ref, o_ref, state_ref, *, n_chunks):
    @pl.when(pl.program_id(1) == 0)
    def _():
        state_ref[...] = jnp.zeros_like(state_ref)

    ri = lax.broadcasted_iota(jnp.int32, (CHUNK, CHUNK), 0)
    ci = lax.broadcasted_iota(jnp.int32, (CHUNK, CHUNK), 1)
    tri = (ci <= ri).astype(F32)

    for c in range(n_chunks):
        rows = pl.ds(c * CHUNK, CHUNK)
        la = la_ref[0, rows, :]
        cum = jnp.dot(tri, la, precision=lax.Precision.HIGHEST, preferred_element_type=F32)
        tot = cum[CHUNK - 1:CHUNK, :]
        k_dec = (k_ref[0, rows, :] * jnp.exp(tot - cum)).astype(BF16)
        decay = jnp.exp(tot)
        for hd in range(GLA_HEADS):
            ks = slice(hd * GLA_DK, (hd + 1) * GLA_DK)
            vs = slice(hd * GLA_DV, (hd + 1) * GLA_DV)
            upd_t = _dot_tn(v_ref[0, rows, vs], k_dec[:, ks])
            st = state_ref[hd] * decay[:, ks] + upd_t
            state_ref[hd] = st
            o = _dot_nt(q_ref[0, rows, ks], st.astype(BF16))
            mu = jnp.mean(o, -1, keepdims=True)
            d = o - mu
            var = jnp.mean(d * d, -1, keepdims=True)
            o = d * lax.rsqrt(var + EPS) * on_ref[:, vs]
            r = r_ref[0, rows, vs]
            o_ref[0, rows, vs] = (o * (r * jax.nn.sigmoid(r))).astype(o_ref.dtype)


def _gla_chunk(q, k, v, r, la, o_norm, *, tc):
    B, S, _ = q.shape
    row = lambda b, i: (b, i, 0)
    return pl.pallas_call(
        functools.partial(_gla_chunk_kernel, n_chunks=tc // CHUNK),
        grid=(B, S // tc),
        in_specs=[pl.BlockSpec((1, tc, GLA_QK), row), pl.BlockSpec((1, tc, GLA_QK), row),
                  pl.BlockSpec((1, tc, GLA_VD), row), pl.BlockSpec((1, tc, GLA_VD), row),
                  pl.BlockSpec((1, tc, GLA_QK), row), _const_spec(o_norm.shape)],
        out_specs=pl.BlockSpec((1, tc, GLA_VD), row),
        out_shape=jax.ShapeDtypeStruct((B, S, GLA_VD), BF16),
        scratch_shapes=[pltpu.VMEM((GLA_HEADS, GLA_DV, GLA_DK), F32)],
        compiler_params=_params(("parallel", "arbitrary")),
        name="gla_chunk",
    )(q, k, v, r, la, o_norm)


def _out_ln_kernel(o_ref, w_ref, x_ref, g_ref, b_ref, y_ref):
    m = _dot(o_ref[...], w_ref[...])
    y_ref[...] = _layer_norm(DN_ALPHA * x_ref[...] + m, g_ref[...], b_ref[...])


def _out_ln(o, w, x, g, b, *, tm):
    T, K = o.shape
    D = x.shape[-1]
    row = lambda i: (i, 0)
    return pl.pallas_call(
        _out_ln_kernel,
        grid=(T // tm,),
        in_specs=[pl.BlockSpec((tm, K), row), _const_spec(w.shape), pl.BlockSpec((tm, D), row),
                  _const_spec(g.shape), _const_spec(b.shape)],
        out_specs=pl.BlockSpec((tm, D), row),
        out_shape=jax.ShapeDtypeStruct((T, D), F32),
        compiler_params=_params(("parallel",)),
        name="out_ln",
    )(o, w, x, g, b)


def _causal_conv3(h, halo, cw, cb):
    w0, w1, w2 = cw[0:1], cw[1:2], cw[2:3]
    y = pltpu.roll(h, 2, axis=0) * w0 + pltpu.roll(h, 1, axis=0) * w1 + h * w2 + cb
    head = jnp.concatenate([halo, h[:SUBLANES]], axis=0)
    y_head = (pltpu.roll(head, 2, axis=0) * w0 + pltpu.roll(head, 1, axis=0) * w1 + head * w2 + cb)
    return jnp.concatenate([y_head[SUBLANES:], y[SUBLANES:]], axis=0)


def _gelu_tanh(g):
    c = math.sqrt(2.0 / math.pi)
    return 0.5 * g * (1.0 + jnp.tanh(c * (g + 0.044715 * (g * g * g))))


def _ffn_ple_kernel(x_ref, p_ref, wu_ref, wg_ref, cwu_ref, cwg_ref, cbu_ref, cbg_ref, wd_ref,
                    g_ref, b_ref, wgate_ref, bgate_ref, wproj_ref, y_ref,
                    halo_u_ref, halo_g_ref, *, n_ff):
    x = x_ref[0]
    xb = x.astype(BF16)
    tm = x.shape[0]

    @pl.when(pl.program_id(1) == 0)
    def _():
        halo_u_ref[...] = jnp.zeros_like(halo_u_ref)
        halo_g_ref[...] = jnp.zeros_like(halo_g_ref)

    acc = None
    h_next = (_dot(xb, wu_ref[0]), _dot(xb, wg_ref[0]))
    for c in range(n_ff):
        hu, hg = h_next
        if c + 1 < n_ff:
            h_next = (_dot(xb, wu_ref[c + 1]), _dot(xb, wg_ref[c + 1]))
        halo_u = halo_u_ref[c]
        halo_g = halo_g_ref[c]
        halo_u_ref[c] = hu[tm - SUBLANES:]
        halo_g_ref[c] = hg[tm - SUBLANES:]
        u = _causal_conv3(hu, halo_u, cwu_ref[c], cbu_ref[c])
        g = _causal_conv3(hg, halo_g, cwg_ref[c], cbg_ref[c])
        part = _dot((u * _gelu_tanh(g)).astype(BF16), wd_ref[c])
        acc = part if acc is None else acc + part
    x2 = _layer_norm(DN_ALPHA * x + acc, g_ref[...], b_ref[...])
    gate = jax.nn.sigmoid(_dot(x2.astype(BF16), wgate_ref[...]) + bgate_ref[...])
    y_ref[0] = x2 + gate * _dot(p_ref[0].astype(BF16), wproj_ref[...])


def _ffn_ple(x, p, wu, wg, cwu, cwg, cbu, cbg, wd, g, b, wgate, bgate, wproj, *, tm):
    B, S, D = x.shape
    n_ff, _, ck = wu.shape
    row = lambda bi, i: (bi, i, 0)
    consts = (wu, wg, cwu, cwg, cbu, cbg, wd, g, b, wgate, bgate, wproj)
    return pl.pallas_call(
        functools.partial(_ffn_ple_kernel, n_ff=n_ff),
        grid=(B, S // tm),
        in_specs=[pl.BlockSpec((1, tm, D), row), pl.BlockSpec((1, tm, p.shape[-1]), row)]
                 + [_const_spec(w.shape) for w in consts],
        out_specs=pl.BlockSpec((1, tm, D), row),
        out_shape=jax.ShapeDtypeStruct((B, S, D), F32),
        scratch_shapes=[pltpu.VMEM((n_ff, SUBLANES, ck), F32),
                        pltpu.VMEM((n_ff, SUBLANES, ck), F32)],
        compiler_params=_params(("parallel", "arbitrary")),
        name="ffn_ple",
    )(x, p, *consts)


FF_CHUNK = 256
TM_MLA_PROJ = 256
TQ_ATTN = 512
HEADS_PER_STEP = 4
TM_GLA_PROJ = 512
TC_GLA = 512
TM_OUT = 512
TM_FFN = 512


def _chunk_cols(w, n):
    K = w.shape[0]
    return w.reshape(K, n, -1).transpose(1, 0, 2)


def kernel(x, p, positions, mla_w_in, mla_q_norm, mla_kv_norm, mla_w_uq, mla_w_uk, mla_w_uv, mla_w_o, gla_w_in, gla_w_a2, gla_b_a, gla_o_norm, gla_w_o, ln1_g, ln1_b, ln2_g, ln2_b, ffn_w_up, ffn_conv_w, ffn_conv_b, ffn_w_down, ple_w_proj, ple_w_gate, ple_b_gate):
    B, S, D = x.shape
    T = B * S
    posf = positions.astype(F32).reshape(B, S, 1)
    inv = 1.0 / (ROPE_THETA ** (jnp.arange(0, MLA_ROPE, 2, dtype=F32) / MLA_ROPE))
    invf = jnp.tile(inv, LANES // inv.shape[0]).reshape(1, LANES)
    n_ff = D_FF // FF_CHUNK

    for i in range(DEPTH):
        j = i // 2
        if i % 2 == 0:
            w_in = jnp.pad(mla_w_in[j], ((0, 0), (0, LANES - MLA_ROPE))).astype(BF16)
            w_uq = mla_w_uq[j].reshape(MLA_Q_LORA, MLA_HEADS, MLA_QK)
            w_qn = w_uq[:, :, :MLA_NOPE].reshape(MLA_Q_LORA, -1).astype(BF16)
            w_qr = w_uq[:, :, MLA_NOPE:].reshape(MLA_Q_LORA, -1).astype(BF16)
            q, k, v = _mla_proj(x, posf, invf, w_in, mla_q_norm[j].reshape(1, -1),
                                mla_kv_norm[j].reshape(1, -1), w_qn, w_qr,
                                mla_w_uk[j].astype(BF16), mla_w_uv[j].astype(BF16), tm=TM_MLA_PROJ)
            o = _mla_attn(q, k, v, positions, tq=TQ_ATTN, hg=HEADS_PER_STEP)
            w_o = mla_w_o[j]
        else:
            w = gla_w_in[j]
            w_q, w_k, w_v, w_r, w_a = jnp.split(
                w, [GLA_QK, 2 * GLA_QK, 2 * GLA_QK + GLA_VD, 2 * GLA_QK + 2 * GLA_VD], axis=1)
            w_a = jnp.pad(w_a, ((0, 0), (0, LANES - GLA_GATE_RANK)))
            w_a2 = jnp.pad(gla_w_a2[j], ((0, LANES - GLA_GATE_RANK), (0, 0)))
            q, k, v, r, la = _gla_proj(x, w_q.astype(BF16), w_k.astype(BF16), w_v.astype(BF16),
                                       w_r.astype(BF16), w_a.astype(BF16), w_a2.astype(BF16),
                                       gla_b_a[j].reshape(1, -1), tm=TM_GLA_PROJ)
            o = _gla_chunk(q, k, v, r, la, gla_o_norm[j].reshape(1, -1), tc=TC_GLA)
            w_o = gla_w_o[j]
        x = _out_ln(o.reshape(T, -1), w_o.astype(BF16), x.reshape(T, D), ln1_g[i].reshape(1, -1),
                    ln1_b[i].reshape(1, -1), tm=TM_OUT).reshape(B, S, D)

        w_up = ffn_w_up[i].astype(BF16)
        cw, cb = ffn_conv_w[i], ffn_conv_b[i].reshape(1, -1)
        x = _ffn_ple(x, p[i],
                     _chunk_cols(w_up[:, :D_FF], n_ff), _chunk_cols(w_up[:, D_FF:], n_ff),
                     _chunk_cols(cw[:, :D_FF], n_ff), _chunk_cols(cw[:, D_FF:], n_ff),
                     _chunk_cols(cb[:, :D_FF], n_ff), _chunk_cols(cb[:, D_FF:], n_ff),
                     ffn_w_down[i].astype(BF16).reshape(n_ff, FF_CHUNK, D),
                     ln2_g[i].reshape(1, -1), ln2_b[i].reshape(1, -1),
                     ple_w_gate[i].astype(BF16), ple_b_gate[i].reshape(1, -1),
                     ple_w_proj[i].astype(BF16), tm=TM_FFN)
    return x
```
